```python
import jax, jax.numpy as jnp
from jax import lax
import numpy as np

D_MODEL = 1024
BATCH = 8
SEQ = 2048
DEPTH = 4

GRID_W = 64
CTX_LEN = 256
HEAD_DIM = 64
MIX_HALF = D_MODEL // 2
N_Q_HEADS = MIX_HALF // HEAD_DIM
KV_GROUP = 4
N_KV_HEADS = N_Q_HEADS // KV_GROUP
WINDOW = 128
ATTN_BLOCK = 128
AX_DIM = HEAD_DIM // 2
ROPE_BASE = 10000.0
POOL_WIDTH = MIX_HALF
POOL_WINDOWS = (2, 4, 8, 16)
N_POOL_GROUPS = len(POOL_WINDOWS)
POOL_GROUP = POOL_WIDTH // N_POOL_GROUPS
CONV_WIDTH = 31
MLP_HIDDEN = 4 * D_MODEL
Q_WIDTH = N_Q_HEADS * HEAD_DIM
KV_WIDTH = N_KV_HEADS * HEAD_DIM
IN_WIDTH = POOL_WIDTH + Q_WIDTH + 2 * KV_WIDTH
N_EVEN = (DEPTH + 1) // 2
N_ODD = DEPTH // 2
NORM_EPS = 1e-6
NEG_LOGIT = -1e30

kernel_name = 'hybrid_pool_swa_conformer_dit'


def rms_norm(z, g):
    zf = z.astype(jnp.float32)
    zf = zf * lax.rsqrt(jnp.mean(zf * zf, axis=-1, keepdims=True) + NORM_EPS)
    return (zf * g.astype(jnp.float32)).astype(z.dtype)


def modulate(z, shift, scale):
    return z * (1 + scale) + shift


def rope_angles(S):
    rows = S // GRID_W
    row = jnp.broadcast_to(jnp.arange(rows, dtype=jnp.int32)[:, None], (rows, GRID_W)).reshape(S)
    col = jnp.broadcast_to(jnp.arange(GRID_W, dtype=jnp.int32)[None, :], (rows, GRID_W)).reshape(S)
    inv = ROPE_BASE ** (-jnp.arange(0, AX_DIM, 2, dtype=jnp.float32) / AX_DIM)
    return row.astype(jnp.float32)[:, None] * inv, col.astype(jnp.float32)[:, None] * inv


def rotate_axis(z, ang):
    cos = jnp.cos(ang)[:, None, :]
    sin = jnp.sin(ang)[:, None, :]
    z1, z2 = jnp.split(z, 2, axis=-1)
    return jnp.concatenate([z1 * cos - z2 * sin, z2 * cos + z1 * sin], axis=-1)


def axial_rope(z, ang_r, ang_c):
    zf = z.astype(jnp.float32)
    out = jnp.concatenate([rotate_axis(zf[..., :AX_DIM], ang_r), rotate_axis(zf[..., AX_DIM:], ang_c)], axis=-1)
    return out.astype(z.dtype)


def multiscale_pool(u, pool_w, pool_scale):
    B, S, _ = u.shape
    ug = u.astype(jnp.float32).reshape(B, S, N_POOL_GROUPS, POOL_GROUP)
    cs = jnp.concatenate([jnp.zeros((B, 1, N_POOL_GROUPS, POOL_GROUP), jnp.float32),
                          jnp.cumsum(ug, axis=1)], axis=1)
    t = jnp.arange(S)
    means = []
    for g, w in enumerate(POOL_WINDOWS):
        lo = jnp.clip(t - w // 2, 0, S)
        hi = jnp.clip(t + w - w // 2, 0, S)
        cnt = (hi - lo).astype(jnp.float32)[None, :, None]
        csg = cs[:, :, g]
        means.append((jnp.take(csg, hi, axis=1) - jnp.take(csg, lo, axis=1)) / cnt)
    pooled = jnp.stack(means, axis=2)
    d = (pooled - ug).astype(u.dtype)
    y = jnp.einsum('bsgc,gcd->bsgd', d, pool_w).reshape(B, S, POOL_WIDTH)
    return y * pool_scale


def window_attention(q, k, v, kc, vc, sink):
    B, S = q.shape[0], q.shape[1]
    L = kc.shape[1]
    nb = S // ATTN_BLOCK
    scale = HEAD_DIM ** -0.5
    qb = q.reshape(B, nb, ATTN_BLOCK, N_KV_HEADS, KV_GROUP, HEAD_DIM)
    pad = ((0, 0), (ATTN_BLOCK, ATTN_BLOCK), (0, 0), (0, 0))
    kp = jnp.pad(k, pad).reshape(B, nb + 2, ATTN_BLOCK, N_KV_HEADS, HEAD_DIM)
    vp = jnp.pad(v, pad).reshape(B, nb + 2, ATTN_BLOCK, N_KV_HEADS, HEAD_DIM)
    kw = jnp.concatenate([kp[:, :-2], kp[:, 1:-1], kp[:, 2:]], axis=2)
    vw = jnp.concatenate([vp[:, :-2], vp[:, 1:-1], vp[:, 2:]], axis=2)
    s_band = jnp.einsum('bnqkgd,bnjkd->bnkgqj', qb, kw).astype(jnp.float32) * scale
    qpos = (jnp.arange(nb) * ATTN_BLOCK)[:, None, None] + jnp.arange(ATTN_BLOCK)[None, :, None]
    kpos = (jnp.arange(nb) * ATTN_BLOCK - ATTN_BLOCK)[:, None, None] + jnp.arange(3 * ATTN_BLOCK)[None, None, :]
    valid = (jnp.abs(kpos - qpos) <= WINDOW) & (kpos >= 0) & (kpos < S)
    s_band = jnp.where(valid[None, :, None, None], s_band, NEG_LOGIT)
    s_ctx = jnp.einsum('bnqkgd,blkd->bnkgql', qb, kc).astype(jnp.float32) * scale
    s_sink = jnp.broadcast_to(sink.astype(jnp.float32).reshape(1, 1, N_KV_HEADS, KV_GROUP, 1, 1),
                              s_ctx.shape[:-1] + (1,))
    p = jax.nn.softmax(jnp.concatenate([s_sink, s_ctx, s_band], axis=-1), axis=-1)
    p_ctx = p[..., 1:1 + L].astype(v.dtype)
    p_band = p[..., 1 + L:].astype(v.dtype)
    out = (jnp.einsum('bnkgql,blkd->bnqkgd', p_ctx, vc)
           + jnp.einsum('bnkgqj,bnjkd->bnqkgd', p_band, vw))
    return out.reshape(B, S, Q_WIDTH)


def context_attention(qc, kc, vc, sink):
    B, L = qc.shape[0], qc.shape[1]
    qg = qc.reshape(B, L, N_KV_HEADS, KV_GROUP, HEAD_DIM)
    s = jnp.einsum('blkgd,bmkd->bkglm', qg, kc).astype(jnp.float32) * (HEAD_DIM ** -0.5)
    s_sink = jnp.broadcast_to(sink.astype(jnp.float32).reshape(1, N_KV_HEADS, KV_GROUP, 1, 1), s.shape[:-1] + (1,))
    p = jax.nn.softmax(jnp.concatenate([s_sink, s], axis=-1), axis=-1)[..., 1:].astype(vc.dtype)
    return jnp.einsum('bkglm,bmkd->blkgd', p, vc).reshape(B, L, Q_WIDTH)


def even_mixer(hx, hc, w_in, pool_w, pool_scale, sink, w_out, ang_r, ang_c, ctx_out):
    B, S, _ = hx.shape
    L = hc.shape[1]
    px = hx @ w_in
    u = px[..., :POOL_WIDTH]
    q = px[..., POOL_WIDTH:POOL_WIDTH + Q_WIDTH].reshape(B, S, N_Q_HEADS, HEAD_DIM)
    k = px[..., POOL_WIDTH + Q_WIDTH:POOL_WIDTH + Q_WIDTH + KV_WIDTH].reshape(B, S, N_KV_HEADS, HEAD_DIM)
    v = px[..., POOL_WIDTH + Q_WIDTH + KV_WIDTH:].reshape(B, S, N_KV_HEADS, HEAD_DIM)
    q = axial_rope(q, ang_r, ang_c)
    k = axial_rope(k, ang_r, ang_c)
    pkv = hc @ w_in[:, POOL_WIDTH + Q_WIDTH:]
    kc = pkv[..., :KV_WIDTH].reshape(B, L, N_KV_HEADS, HEAD_DIM)
    vc = pkv[..., KV_WIDTH:].reshape(B, L, N_KV_HEADS, HEAD_DIM)
    y_pool = multiscale_pool(u, pool_w, pool_scale)
    y_attn = window_attention(q, k, v, kc, vc, sink)
    out_x = jnp.concatenate([y_pool, y_attn], axis=-1) @ w_out
    if not ctx_out:
        return out_x, None
    pc = hc @ w_in[:, :POOL_WIDTH + Q_WIDTH]
    uc = pc[..., :POOL_WIDTH]
    qc = pc[..., POOL_WIDTH:].reshape(B, L, N_Q_HEADS, HEAD_DIM)
    yc = jnp.concatenate([multiscale_pool(uc, pool_w, pool_scale), context_attention(qc, kc, vc, sink)], axis=-1)
    return out_x, yc @ w_out


def conformer_conv(h, w_pw1, w_dw, b_dw, ln_g, ln_b, w_pw2):
    a = h @ w_pw1
    a1, a2 = jnp.split(a, 2, axis=-1)
    g = a1 * jax.nn.sigmoid(a2)
    half = CONV_WIDTH // 2
    y = lax.conv_general_dilated(g, w_dw[:, None, :], window_strides=(1,), padding=[(half, half)],
                                 dimension_numbers=('NWC', 'WIO', 'NWC'),
                                 feature_group_count=D_MODEL) + b_dw
    yf = y.astype(jnp.float32)
    mu = jnp.mean(yf, axis=-1, keepdims=True)
    var = jnp.mean(jnp.square(yf - mu), axis=-1, keepdims=True)
    yf = (yf - mu) * lax.rsqrt(var + NORM_EPS) * ln_g.astype(jnp.float32) + ln_b.astype(jnp.float32)
    y = (yf * jax.nn.sigmoid(yf)).astype(h.dtype)
    return y @ w_pw2


def sq_relu_mlp(h, w1, w2):
    return jnp.square(jax.nn.relu(h @ w1)) @ w2


def setup_inputs(seed: int = 0) -> dict:
    key = jax.random.key(seed)
    ks = jax.random.split(key, 24)
    f32 = jnp.float32
    D = D_MODEL

    def nrm(k, shape, s):
        return jax.random.normal(k, shape, f32) * s

    return {
        'x': nrm(ks[0], (BATCH, SEQ, D), 1.0),
        'c': nrm(ks[1], (BATCH, D), 1.0),
        'ctx': nrm(ks[2], (BATCH, CTX_LEN, D), 1.0),
        'c_ctx': nrm(ks[3], (D,), 1.0),
        'w_mod': nrm(ks[4], (DEPTH, D, 6 * D), 0.5 * D ** -0.5),
        'b_mod': nrm(ks[5], (DEPTH, 6 * D), 0.02),
        'norm1_g': 1.0 + nrm(ks[6], (DEPTH, D), 0.02),
        'norm2_g': 1.0 + nrm(ks[7], (DEPTH, D), 0.02),
        'mix_w_in': nrm(ks[8], (N_EVEN, D, IN_WIDTH), D ** -0.5),
        'pool_w': nrm(ks[9], (N_EVEN, N_POOL_GROUPS, POOL_GROUP, POOL_GROUP), POOL_GROUP ** -0.5),
        'pool_scale': 1.0 + nrm(ks[10], (N_EVEN, POOL_WIDTH), 0.02),
        'attn_sink': nrm(ks[11], (N_EVEN, N_Q_HEADS), 0.5),
        'mix_w_out': nrm(ks[12], (N_EVEN, POOL_WIDTH + Q_WIDTH, D), (POOL_WIDTH + Q_WIDTH) ** -0.5),
        'conv_w_pw1': nrm(ks[13], (N_ODD, D, 2 * D), D ** -0.5),
        'conv_w_dw': nrm(ks[14], (N_ODD, CONV_WIDTH, D), CONV_WIDTH ** -0.5),
        'conv_b_dw': nrm(ks[15], (N_ODD, D), 0.02),
        'conv_ln_g': 1.0 + nrm(ks[16], (N_ODD, D), 0.02),
        'conv_ln_b': nrm(ks[17], (N_ODD, D), 0.02),
        'conv_w_pw2': nrm(ks[18], (N_ODD, D, D), D ** -0.5),
        'mlp_w1': nrm(ks[19], (DEPTH, D, MLP_HIDDEN), D ** -0.5),
        'mlp_w2': nrm(ks[20], (DEPTH, MLP_HIDDEN, D), MLP_HIDDEN ** -0.5),
        'final_g': 1.0 + nrm(ks[21], (D,), 0.02),
    }


def reference(x, c, ctx, c_ctx, w_mod, b_mod, norm1_g, norm2_g, mix_w_in, pool_w, pool_scale,
              attn_sink, mix_w_out, conv_w_pw1, conv_w_dw, conv_b_dw, conv_ln_g, conv_ln_b,
              conv_w_pw2, mlp_w1, mlp_w2, final_g):
    S = x.shape[1]
    ang_r, ang_c = rope_angles(S)
    silu_c = jax.nn.silu(c)
    silu_cc = jax.nn.silu(c_ctx)
    last_reader = ((DEPTH - 1) // 2) * 2
    xc = ctx
    for l in range(DEPTH):
        mod = silu_c @ w_mod[l] + b_mod[l]
        sh1, sc1, g1, sh2, sc2, g2 = [m[:, None, :] for m in jnp.split(mod, 6, axis=-1)]
        need_ctx = l <= last_reader
        upd_ctx = l < last_reader
        if need_ctx:
            sh1c, sc1c, g1c, sh2c, sc2c, g2c = jnp.split(silu_cc @ w_mod[l] + b_mod[l], 6, axis=-1)
        hx = modulate(rms_norm(x, norm1_g[l]), sh1, sc1)
        if l % 2 == 0:
            i = l // 2
            hc = modulate(rms_norm(xc, norm1_g[l]), sh1c, sc1c)
            out_x, out_c = even_mixer(hx, hc, mix_w_in[i], pool_w[i], pool_scale[i], attn_sink[i],
                                      mix_w_out[i], ang_r, ang_c, upd_ctx)
        else:
            i = l // 2
            conv_args = (conv_w_pw1[i], conv_w_dw[i], conv_b_dw[i], conv_ln_g[i], conv_ln_b[i], conv_w_pw2[i])
            out_x = conformer_conv(hx, *conv_args)
            if upd_ctx:
                hc = modulate(rms_norm(xc, norm1_g[l]), sh1c, sc1c)
                out_c = conformer_conv(hc, *conv_args)
        x = x + g1 * out_x
        x = x + g2 * sq_relu_mlp(modulate(rms_norm(x, norm2_g[l]), sh2, sc2), mlp_w1[l], mlp_w2[l])
        if upd_ctx:
            xc = xc + g1c * out_c
            xc = xc + g2c * sq_relu_mlp(modulate(rms_norm(xc, norm2_g[l]), sh2c, sc2c), mlp_w1[l], mlp_w2[l])
    return rms_norm(x, final_g)
```

```python
import functools

import jax
import jax.numpy as jnp
from jax import lax
from jax.experimental import pallas as pl
from jax.experimental.pallas import tpu as pltpu

F32 = jnp.float32
BF16 = jnp.bfloat16

D_MODEL = 1024
DEPTH = 4
GRID_W = 64
HEAD_DIM = 64
MIX_HALF = D_MODEL // 2
N_Q_HEADS = MIX_HALF // HEAD_DIM
KV_GROUP = 4
N_KV_HEADS = N_Q_HEADS // KV_GROUP
WINDOW = 128
ATTN_BLOCK = 128
AX_DIM = HEAD_DIM // 2
ROPE_BASE = 10000.0
POOL_WINDOWS = (2, 4, 8, 16)
POOL_GROUP = MIX_HALF // len(POOL_WINDOWS)
CONV_WIDTH = 31
MLP_HIDDEN = 4 * D_MODEL
Q_WIDTH = N_Q_HEADS * HEAD_DIM
KV_WIDTH = N_KV_HEADS * HEAD_DIM
NORM_EPS = 1e-6
NEG_LOGIT = -1e30

LANES = 128
SUBLANES = 8
POOL_HALO = 8
CONV_HALO = 16
MOD_ROWS = 16
VMEM_LIMIT_BIG = 56 * 1024 * 1024
VMEM_LIMIT_SMALL = 40 * 1024 * 1024

ROW_TILE = 512
HIDDEN_CHUNK = 1024


def _resident(shape):
    nd = len(shape)
    return pl.BlockSpec(shape, lambda *_: (0,) * nd, pipeline_mode=pl.Buffered(1))


def _norm_mod(x, g, shift, scale):
    ms = jnp.mean(x * x, axis=-1, keepdims=True)
    xn = x * lax.rsqrt(ms + NORM_EPS) * g
    return xn * (1.0 + scale) + shift


def _mod_kernel(c_ref, w_ref, b_ref, o_ref):
    c = c_ref[...]
    sc = (c * jax.nn.sigmoid(c)).astype(BF16)
    w = w_ref[0].astype(BF16)
    o_ref[0] = jnp.dot(sc, w, preferred_element_type=F32) + b_ref[0]


def _mod_call(cpad, w_mod, b_mod):
    depth, d, n = w_mod.shape
    tn = 1536
    return pl.pallas_call(
        _mod_kernel,
        grid=(depth, n // tn),
        in_specs=[
            pl.BlockSpec((MOD_ROWS, d), lambda l, j: (0, 0)),
            pl.BlockSpec((1, d, tn), lambda l, j: (l, 0, j)),
            pl.BlockSpec((1, 1, tn), lambda l, j: (l, 0, j)),
        ],
        out_specs=pl.BlockSpec((1, MOD_ROWS, tn), lambda l, j: (l, 0, j)),
        out_shape=jax.ShapeDtypeStruct((depth, MOD_ROWS, n), F32),
        compiler_params=pltpu.CompilerParams(
            dimension_semantics=("arbitrary", "arbitrary"), vmem_limit_bytes=VMEM_LIMIT_SMALL),
        name="adaln_mod",
    )(cpad, w_mod, b_mod.reshape(depth, 1, n))


def _rope_slab(z, cos, sin_signed, first_half):
    partner = jnp.where(first_half, pltpu.roll(z, LANES - AX_DIM // 2, 1), pltpu.roll(z, AX_DIM // 2, 1))
    return z * cos + partner * sin_signed


def _in_proj_kernel(*refs, rope, has_uq):
    if rope:
        x_ref, mod_ref, g_ref, w_ref, cos_ref, sin_ref = refs[:6]
        outs = refs[6:]
    else:
        x_ref, mod_ref, g_ref, w_ref = refs[:4]
        outs = refs[4:]
    h = _norm_mod(x_ref[...], g_ref[...], mod_ref[0, 0:1, :], mod_ref[0, 1:2, :]).astype(BF16)
    px = jnp.dot(h, w_ref[...], preferred_element_type=F32)
    if has_uq:
        u_ref, q_ref, k_ref, v_ref = outs
        u_ref[...] = px[:, :MIX_HALF]
        off = MIX_HALF
    else:
        k_ref, v_ref = outs
        off = 0
    if rope:
        cos = cos_ref[...]
        sin = sin_ref[...]
        lane = lax.broadcasted_iota(jnp.int32, cos.shape, 1)
        first_half = (lane % AX_DIM) < (AX_DIM // 2)
        rot = lambda z: _rope_slab(z, cos, sin, first_half)
    else:
        rot = lambda z: z
    if has_uq:
        for s in range(Q_WIDTH // LANES):
            q_ref[:, s * LANES:(s + 1) * LANES] = rot(px[:, off + s * LANES:off + (s + 1) * LANES]).astype(BF16)
        off += Q_WIDTH
    k_ref[...] = rot(px[:, off:off + KV_WIDTH]).astype(BF16)
    v_ref[...] = px[:, off + KV_WIDTH:off + 2 * KV_WIDTH].astype(BF16)


def _in_proj_call(xt, mod, g, w, rope_tabs, *, tm, tiles_per_mod, has_uq):
    t, d = xt.shape
    nw = w.shape[1]
    rope = rope_tabs is not None
    tile = lambda n: pl.BlockSpec((tm, n), lambda i: (i, 0))
    in_specs = [
        tile(d),
        pl.BlockSpec((1, 6, d), lambda i: (i // tiles_per_mod, 0, 0)),
        _resident((1, d)),
        _resident((d, nw)),
    ]
    args = [xt, mod, g, w]
    if rope:
        tabs_per_seq = rope_tabs[0].shape[0] // tm
        in_specs += [pl.BlockSpec((tm, LANES), lambda i: (i % tabs_per_seq, 0))] * 2
        args += list(rope_tabs)
    out_specs = [tile(KV_WIDTH), tile(KV_WIDTH)]
    out_shape = [jax.ShapeDtypeStruct((t, KV_WIDTH), BF16)] * 2
    if has_uq:
        out_specs = [tile(MIX_HALF), tile(Q_WIDTH)] + out_specs
        out_shape = [jax.ShapeDtypeStruct((t, MIX_HALF), F32), jax.ShapeDtypeStruct((t, Q_WIDTH), BF16)] + out_shape
    return pl.pallas_call(
        functools.partial(_in_proj_kernel, rope=rope, has_uq=has_uq),
        grid=(t // tm,),
        in_specs=in_specs,
        out_specs=out_specs,
        out_shape=out_shape,
        compiler_params=pltpu.CompilerParams(
            dimension_semantics=("arbitrary",), vmem_limit_bytes=VMEM_LIMIT_SMALL),
        name="mixer_in_proj",
    )(*args)


def _attn_kernel(*refs, band, seq_len):
    if band:
        q_ref, k_ref, v_ref, kc_ref, vc_ref, sink_ref, o_ref = refs
    else:
        q_ref, kc_ref, vc_ref, sink_ref, o_ref = refs
    scale = HEAD_DIM ** -0.5
    q = q_ref[...]
    kc = kc_ref[...]
    vc = vc_ref[...]
    contract_last = (((1,), (1,)), ((), ()))
    if band:
        n = pl.program_id(1)
        span = 3 * ATTN_BLOCK
        start = pl.multiple_of(jnp.clip(n * ATTN_BLOCK - ATTN_BLOCK, 0, seq_len - span), ATTN_BLOCK)
        kw = k_ref[pl.ds(start, span), :]
        vw = v_ref[pl.ds(start, span), :]
        qpos = n * ATTN_BLOCK + lax.broadcasted_iota(jnp.int32, (ATTN_BLOCK, span), 0)
        kpos = start + lax.broadcasted_iota(jnp.int32, (ATTN_BLOCK, span), 1)
        valid = jnp.abs(kpos - qpos) <= WINDOW
    for kvh in range(N_KV_HEADS):
        ksl = slice(kvh * HEAD_DIM, (kvh + 1) * HEAD_DIM)
        for g in range(KV_GROUP):
            hq = kvh * KV_GROUP + g
            qh = q[:, hq * HEAD_DIM:(hq + 1) * HEAD_DIM]
            sink = sink_ref[hq]
            s_c = lax.dot_general(qh, kc[:, ksl], contract_last, preferred_element_type=F32) * scale
            m = jnp.maximum(jnp.max(s_c, axis=-1, keepdims=True), sink)
            if band:
                s_b = lax.dot_general(qh, kw[:, ksl], contract_last, preferred_element_type=F32) * scale
                s_b = jnp.where(valid, s_b, NEG_LOGIT)
                m = jnp.maximum(m, jnp.max(s_b, axis=-1, keepdims=True))
            p_c = jnp.exp(s_c - m)
            denom = jnp.sum(p_c, axis=-1, keepdims=True) + jnp.exp(sink - m)
            o = jnp.dot(p_c.astype(BF16), vc[:, ksl], preferred_element_type=F32)
            if band:
                p_b = jnp.exp(s_b - m)
                denom = denom + jnp.sum(p_b, axis=-1, keepdims=True)
                o = o + jnp.dot(p_b.astype(BF16), vw[:, ksl], preferred_element_type=F32)
            o_ref[:, hq * HEAD_DIM:(hq + 1) * HEAD_DIM] = (o / denom).astype(BF16)


def _attn_call(q, k, v, kc, vc, sink, *, batch, seq_len, ctx_len, band):
    blocks = seq_len // ATTN_BLOCK
    qspec = pl.BlockSpec((ATTN_BLOCK, Q_WIDTH), lambda b, n: (b * blocks + n, 0))
    seq_spec = pl.BlockSpec((seq_len, KV_WIDTH), lambda b, n: (b, 0))
    ctx_spec = pl.BlockSpec((ctx_len, KV_WIDTH), lambda b, n: (b, 0))
    smem = pl.BlockSpec(memory_space=pltpu.SMEM)
    if band:
        in_specs = [qspec, seq_spec, seq_spec, ctx_spec, ctx_spec, smem]
        args = (q, k, v, kc, vc, sink)
    else:
        in_specs = [qspec, ctx_spec, ctx_spec, smem]
        args = (q, kc, vc, sink)
    return pl.pallas_call(
        functools.partial(_attn_kernel, band=band, seq_len=seq_len),
        grid=(batch, blocks),
        in_specs=in_specs,
        out_specs=qspec,
        out_shape=jax.ShapeDtypeStruct(q.shape, BF16),
        compiler_params=pltpu.CompilerParams(
            dimension_semantics=("arbitrary", "arbitrary"), vmem_limit_bytes=VMEM_LIMIT_SMALL),
        name="window_attn" if band else "ctx_attn",
    )(*args)


def _mlp_tail(x1, mod_ref, n2g_ref, w1_ref, w2_ref, h_ref, fg_ref):
    hm = _norm_mod(x1, n2g_ref[...], mod_ref[0, 3:4, :], mod_ref[0, 4:5, :]).astype(BF16)
    for c in range(MLP_HIDDEN // HIDDEN_CHUNK):
        cols = slice(c * HIDDEN_CHUNK, (c + 1) * HIDDEN_CHUNK)
        a = jnp.maximum(jnp.dot(hm, w1_ref[:, cols], preferred_element_type=F32), 0.0)
        h_ref[:, cols] = (a * a).astype(BF16)
    y = jnp.dot(h_ref[...], w2_ref[...], preferred_element_type=F32)
    x2 = x1 + mod_ref[0, 5:6, :] * y
    if fg_ref is not None:
        ms = jnp.mean(x2 * x2, axis=-1, keepdims=True)
        x2 = x2 * lax.rsqrt(ms + NORM_EPS) * fg_ref[...]
    return x2


def _halo_masks(tiles_per_seq):
    j = pl.program_id(0) % tiles_per_seq
    return j, (j > 0).astype(F32), (j < tiles_per_seq - 1).astype(F32)


def _mix_out_mlp_kernel(*refs, tm, tiles_per_seq, seq_len, final):
    (x_ref, u_ref, up_ref, un_ref, y_ref, mod_ref, n2g_ref, pw_ref, ps_ref, wo_ref, w1_ref, w2_ref) = refs[:12]
    fg_ref = refs[12] if final else None
    o_ref, h_ref = refs[-2:]
    j, has_prev, has_next = _halo_masks(tiles_per_seq)
    u = u_ref[...]
    ext = jnp.concatenate([up_ref[...] * has_prev, u, un_ref[...] * has_next], axis=0)
    rows = tm + 2 * POOL_HALO
    t = j * tm + lax.broadcasted_iota(jnp.int32, (tm, 1), 0)
    pooled = []
    for gi, w in enumerate(POOL_WINDOWS):
        e = ext[:, gi * POOL_GROUP:(gi + 1) * POOL_GROUP]
        s = e + pltpu.roll(e, 1, 0)
        step = 1
        while 2 * step < w:
            s = pltpu.roll(s, step, 0) + pltpu.roll(s, rows - step, 0)
            step *= 2
        cnt = (jnp.minimum(t + w // 2, seq_len) - jnp.maximum(t - w // 2, 0)).astype(F32)
        mean = s[POOL_HALO:POOL_HALO + tm] / cnt
        d = (mean - u[:, gi * POOL_GROUP:(gi + 1) * POOL_GROUP]).astype(BF16)
        pooled.append(jnp.dot(d, pw_ref[gi], preferred_element_type=F32))
    y_pool = jnp.concatenate(pooled, axis=-1) * ps_ref[...]
    cat = jnp.concatenate([y_pool.astype(BF16), y_ref[...]], axis=-1)
    x1 = x_ref[...] + mod_ref[0, 2:3, :] * jnp.dot(cat, wo_ref[...], preferred_element_type=F32)
    o_ref[...] = _mlp_tail(x1, mod_ref, n2g_ref, w1_ref, w2_ref, h_ref, fg_ref)


def _mix_out_mlp_call(xt, u, y, mod, n2g, pool_w, pool_scale, w_out, w1, w2, final_g, *, tm, tiles_per_mod, seq_len):
    t, d = xt.shape
    tiles_per_seq = seq_len // tm
    hb = tm // POOL_HALO
    last_hb = t // POOL_HALO - 1
    tile = lambda n: pl.BlockSpec((tm, n), lambda i: (i, 0))
    in_specs = [
        tile(d),
        tile(MIX_HALF),
        pl.BlockSpec((POOL_HALO, MIX_HALF), lambda i: (jnp.maximum(i * hb - 1, 0), 0)),
        pl.BlockSpec((POOL_HALO, MIX_HALF), lambda i: (jnp.minimum((i + 1) * hb, last_hb), 0)),
        tile(Q_WIDTH),
        pl.BlockSpec((1, 6, d), lambda i: (i // tiles_per_mod, 0, 0)),
        _resident((1, d)),
        _resident(pool_w.shape),
        _resident((1, MIX_HALF)),
        _resident(w_out.shape),
        _resident(w1.shape),
        _resident(w2.shape),
    ]
    args = [xt, u, u, u, y, mod, n2g, pool_w, pool_scale, w_out, w1, w2]
    if final_g is not None:
        in_specs.append(_resident((1, d)))
        args.append(final_g)
    return pl.pallas_call(
        functools.partial(_mix_out_mlp_kernel, tm=tm, tiles_per_seq=tiles_per_seq, seq_len=seq_len,
                          final=final_g is not None),
        grid=(t // tm,),
        in_specs=in_specs,
        out_specs=tile(d),
        out_shape=jax.ShapeDtypeStruct((t, d), F32),
        scratch_shapes=[pltpu.VMEM((tm, MLP_HIDDEN), BF16)],
        compiler_params=pltpu.CompilerParams(
            dimension_semantics=("arbitrary",), vmem_limit_bytes=VMEM_LIMIT_BIG),
        name="mixer_out_mlp",
    )(*args)


def _glu_kernel(x_ref, mod_ref, g_ref, w_ref, o_ref):
    h = _norm_mod(x_ref[...], g_ref[...], mod_ref[0, 0:1, :], mod_ref[0, 1:2, :]).astype(BF16)
    a = jnp.dot(h, w_ref[...], preferred_element_type=F32)
    o_ref[...] = a[:, :D_MODEL] * jax.nn.sigmoid(a[:, D_MODEL:])


def _glu_call(xt, mod, g, w, *, tm, tiles_per_mod):
    t, d = xt.shape
    tile = pl.BlockSpec((tm, d), lambda i: (i, 0))
    return pl.pallas_call(
        _glu_kernel,
        grid=(t // tm,),
        in_specs=[tile, pl.BlockSpec((1, 6, d), lambda i: (i // tiles_per_mod, 0, 0)),
                  _resident((1, d)), _resident(w.shape)],
        out_specs=tile,
        out_shape=jax.ShapeDtypeStruct((t, d), F32),
        compiler_params=pltpu.CompilerParams(
            dimension_semantics=("arbitrary",), vmem_limit_bytes=VMEM_LIMIT_SMALL),
        name="conv_glu",
    )(xt, mod, g, w)


def _conv_mlp_kernel(*refs, tm, tiles_per_seq, final):
    (x_ref, g_ref, gp_ref, gn_ref, mod_ref, n2g_ref, wdw_ref, bdw_ref, lng_ref, lnb_ref,
     wp2_ref, w1_ref, w2_ref) = refs[:13]
    fg_ref = refs[13] if final else None
    o_ref, ext_ref, h_ref = refs[-3:]
    _, has_prev, has_next = _halo_masks(tiles_per_seq)
    ext_ref[0:CONV_HALO, :] = gp_ref[...] * has_prev
    ext_ref[CONV_HALO:CONV_HALO + tm, :] = g_ref[...]
    ext_ref[CONV_HALO + tm:, :] = gn_ref[...] * has_next
    first = CONV_HALO - CONV_WIDTH // 2
    acc = ext_ref[pl.ds(first, tm), :] * wdw_ref[0:1, :] + bdw_ref[...]
    for k in range(1, CONV_WIDTH):
        acc = acc + ext_ref[pl.ds(first + k, tm), :] * wdw_ref[k:k + 1, :]
    mu = jnp.mean(acc, axis=-1, keepdims=True)
    cen = acc - mu
    var = jnp.mean(cen * cen, axis=-1, keepdims=True)
    yf = cen * lax.rsqrt(var + NORM_EPS) * lng_ref[...] + lnb_ref[...]
    y = (yf * jax.nn.sigmoid(yf)).astype(BF16)
    x1 = x_ref[...] + mod_ref[0, 2:3, :] * jnp.dot(y, wp2_ref[...], preferred_element_type=F32)
    o_ref[...] = _mlp_tail(x1, mod_ref, n2g_ref, w1_ref, w2_ref, h_ref, fg_ref)


def _conv_mlp_call(xt, g, mod, n2g, w_dw, b_dw, ln_g, ln_b, w_pw2, w1, w2, final_g, *, tm, tiles_per_mod, seq_len):
    t, d = xt.shape
    tiles_per_seq = seq_len // tm
    hb = tm // CONV_HALO
    last_hb = t // CONV_HALO - 1
    tile = pl.BlockSpec((tm, d), lambda i: (i, 0))
    in_specs = [
        tile,
        tile,
        pl.BlockSpec((CONV_HALO, d), lambda i: (jnp.maximum(i * hb - 1, 0), 0)),
        pl.BlockSpec((CONV_HALO, d), lambda i: (jnp.minimum((i + 1) * hb, last_hb), 0)),
        pl.BlockSpec((1, 6, d), lambda i: (i // tiles_per_mod, 0, 0)),
        _resident((1, d)),
        _resident(w_dw.shape),
        _resident((1, d)),
        _resident((1, d)),
        _resident((1, d)),
        _resident(w_pw2.shape),
        _resident(w1.shape),
        _resident(w2.shape),
    ]
    args = [xt, g, g, g, mod, n2g, w_dw, b_dw, ln_g, ln_b, w_pw2, w1, w2]
    if final_g is not None:
        in_specs.append(_resident((1, d)))
        args.append(final_g)
    return pl.pallas_call(
        functools.partial(_conv_mlp_kernel, tm=tm, tiles_per_seq=tiles_per_seq, final=final_g is not None),
        grid=(t // tm,),
        in_specs=in_specs,
        out_specs=tile,
        out_shape=jax.ShapeDtypeStruct((t, d), F32),
        scratch_shapes=[pltpu.VMEM((tm + 2 * CONV_HALO, d), F32), pltpu.VMEM((tm, MLP_HIDDEN), BF16)],
        compiler_params=pltpu.CompilerParams(
            dimension_semantics=("arbitrary",), vmem_limit_bytes=VMEM_LIMIT_BIG),
        name="conv_out_mlp",
    )(*args)


def _rope_tables(seq_len):
    pos = jnp.arange(seq_len, dtype=jnp.int32)
    inv = ROPE_BASE ** (-jnp.arange(0, AX_DIM, 2, dtype=F32) / AX_DIM)
    ang_r = (pos // GRID_W).astype(F32)[:, None] * inv
    ang_c = (pos % GRID_W).astype(F32)[:, None] * inv
    cos = jnp.concatenate([jnp.cos(ang_r)] * 2 + [jnp.cos(ang_c)] * 2, axis=-1)
    sin = jnp.concatenate([-jnp.sin(ang_r), jnp.sin(ang_r), -jnp.sin(ang_c), jnp.sin(ang_c)], axis=-1)
    reps = LANES // HEAD_DIM
    return jnp.tile(cos, (1, reps)), jnp.tile(sin, (1, reps))


def kernel(x, c, ctx, c_ctx, w_mod, b_mod, norm1_g, norm2_g, mix_w_in, pool_w, pool_scale, attn_sink, mix_w_out, conv_w_pw1, conv_w_dw, conv_b_dw, conv_ln_g, conv_ln_b, conv_w_pw2, mlp_w1, mlp_w2, final_g):
    batch, seq_len, d = x.shape
    ctx_len = ctx.shape[1]
    depth = w_mod.shape[0]
    xt = x.reshape(batch * seq_len, d)
    ct = ctx.reshape(batch * ctx_len, d)
    tm_x = ROW_TILE
    tm_c = ctx_len
    x_tiles_per_mod = seq_len // tm_x
    c_tiles_per_mod = (batch * ctx_len) // tm_c

    cpad = jnp.concatenate([c, c_ctx[None, :], jnp.zeros((MOD_ROWS - batch - 1, d), F32)], axis=0)
    mod = _mod_call(cpad, w_mod, b_mod)
    rope_tabs = _rope_tables(seq_len)
    row = lambda a: a.reshape(1, -1)

    last_reader = ((depth - 1) // 2) * 2
    for l in range(depth):
        mod_x = mod[l, :batch].reshape(batch, 6, d)
        mod_c = mod[l, batch:batch + 1].reshape(1, 6, d)
        need_ctx = l <= last_reader
        upd_ctx = l < last_reader
        fg = row(final_g) if l == depth - 1 else None
        w1 = mlp_w1[l].astype(BF16)
        w2 = mlp_w2[l].astype(BF16)
        n1g, n2g = row(norm1_g[l]), row(norm2_g[l])
        i = l // 2
        if l % 2 == 0:
            w_in = mix_w_in[i].astype(BF16)
            w_out = mix_w_out[i].astype(BF16)
            pw = pool_w[i].astype(BF16)
            ps = row(pool_scale[i])
            sink = attn_sink[i]
            if upd_ctx:
                uc, qc, kc, vc = _in_proj_call(ct, mod_c, n1g, w_in, None, tm=tm_c,
                                               tiles_per_mod=c_tiles_per_mod, has_uq=True)
            else:
                kc, vc = _in_proj_call(ct, mod_c, n1g, w_in[:, MIX_HALF + Q_WIDTH:], None, tm=tm_c,
                                       tiles_per_mod=c_tiles_per_mod, has_uq=False)
            u, q, k, v = _in_proj_call(xt, mod_x, n1g, w_in, rope_tabs, tm=tm_x,
                                       tiles_per_mod=x_tiles_per_mod, has_uq=True)
            y = _attn_call(q, k, v, kc, vc, sink, batch=batch, seq_len=seq_len, ctx_len=ctx_len, band=True)
            xt = _mix_out_mlp_call(xt, u, y, mod_x, n2g, pw, ps, w_out, w1, w2, fg, tm=tm_x,
                                   tiles_per_mod=x_tiles_per_mod, seq_len=seq_len)
            if upd_ctx:
                yc = _attn_call(qc, None, None, kc, vc, sink, batch=batch, seq_len=ctx_len, ctx_len=ctx_len,
                                band=False)
                ct = _mix_out_mlp_call(ct, uc, yc, mod_c, n2g, pw, ps, w_out, w1, w2, None, tm=tm_c,
                                       tiles_per_mod=c_tiles_per_mod, seq_len=ctx_len)
        else:
            w_pw1 = conv_w_pw1[i].astype(BF16)
            w_pw2 = conv_w_pw2[i].astype(BF16)
            conv_args = (conv_w_dw[i], row(conv_b_dw[i]), row(conv_ln_g[i]), row(conv_ln_b[i]), w_pw2, w1, w2)
            g = _glu_call(xt, mod_x, n1g, w_pw1, tm=tm_x, tiles_per_mod=x_tiles_per_mod)
            xt = _conv_mlp_call(xt, g, mod_x, n2g, *conv_args, fg, tm=tm_x,
                                tiles_per_mod=x_tiles_per_mod, seq_len=seq_len)
            if upd_ctx:
                gc = _glu_call(ct, mod_c, n1g, w_pw1, tm=tm_c, tiles_per_mod=c_tiles_per_mod)
                ct = _conv_mlp_call(ct, gc, mod_c, n2g, *conv_args, None, tm=tm_c,
                                    tiles_per_mod=c_tiles_per_mod, seq_len=ctx_len)
    return xt.reshape(batch, seq_len, d)
```
